```python
import math
import jax, jax.numpy as jnp
from jax import lax
import numpy as np

D_MODEL = 4096
BATCH = 2
SEQ = 4096
DEPTH = 2

HEAD_DIM = 128
MOBA_HEADS = 12
MOBA_WIDTH = MOBA_HEADS * HEAD_DIM
MOBA_BLOCK = 256
MOBA_TOPK = 3
MOBA_QUERY_CHUNK = 32
ROPE_THETA = 500000.0
ROPE_DIMS = HEAD_DIM // 4
CONV_CHANNELS = 1024
CONV_WIDTH = 31
MLA_HEADS = 12
MLA_Q_RANK = 1024
MLA_KV_RANK = 512
MLA_NOPE_DIM = 128
MLA_ROPE_DIM = 64
MLA_V_DIM = 128
MLA_QK_DIM = MLA_NOPE_DIM + MLA_ROPE_DIM
MLA_WIDTH = MLA_HEADS * MLA_V_DIM
MLA_ROPE_THETA = 10000.0
ATTN_QUERY_BLOCK = 128
D_IN = 3 * MOBA_WIDTH + 2 * CONV_CHANNELS + MLA_Q_RANK + MLA_KV_RANK + MLA_ROPE_DIM
IN_SPLITS = [MOBA_WIDTH, 2 * MOBA_WIDTH, 3 * MOBA_WIDTH,
             3 * MOBA_WIDTH + 2 * CONV_CHANNELS,
             3 * MOBA_WIDTH + 2 * CONV_CHANNELS + MLA_Q_RANK,
             3 * MOBA_WIDTH + 2 * CONV_CHANNELS + MLA_Q_RANK + MLA_KV_RANK]
D_MIX = MOBA_WIDTH + CONV_CHANNELS + MLA_WIDTH
N_EXPERTS = 16
N_EXPERT_GROUPS = 4
EXPERTS_PER_GROUP = N_EXPERTS // N_EXPERT_GROUPS
TOP_K = 2
D_EXPERT = 1024

NORM_EPS = 1e-6
NEG_INF = -1e30

kernel_name = "hybrid_moba_conformer_mla_grouped_moe"


def rms_norm(x, g):
    xf = x.astype(jnp.float32)
    y = xf * lax.rsqrt(jnp.mean(xf * xf, axis=-1, keepdims=True) + NORM_EPS)
    return (y * g.astype(jnp.float32)).astype(x.dtype)


def layer_norm(x, g, b):
    xf = x.astype(jnp.float32)
    mu = jnp.mean(xf, axis=-1, keepdims=True)
    var = jnp.mean(jnp.square(xf - mu), axis=-1, keepdims=True)
    y = (xf - mu) * lax.rsqrt(var + NORM_EPS)
    return (y * g.astype(jnp.float32) + b.astype(jnp.float32)).astype(x.dtype)


def partial_rope(x, pos, rot_dims, theta):
    half = rot_dims // 2
    inv_freq = jnp.float32(theta) ** (-jnp.arange(half, dtype=jnp.float32) * (2.0 / rot_dims))
    ang = pos.astype(jnp.float32)[:, :, None, None] * inv_freq
    cos, sin = jnp.cos(ang), jnp.sin(ang)
    xr = x[..., :rot_dims].astype(jnp.float32)
    x1, x2 = xr[..., :half], xr[..., half:]
    rot = jnp.concatenate([x1 * cos - x2 * sin, x2 * cos + x1 * sin], axis=-1).astype(x.dtype)
    return jnp.concatenate([rot, x[..., rot_dims:]], axis=-1)


def moba_attention(q, k, v):
    B, H, S, dh = q.shape
    n_blocks = -(-S // MOBA_BLOCK)
    n_sel = min(MOBA_TOPK, n_blocks)
    pad = n_blocks * MOBA_BLOCK - S
    kp = jnp.pad(k, ((0, 0), (0, 0), (0, pad), (0, 0)))
    vp = jnp.pad(v, ((0, 0), (0, 0), (0, pad), (0, 0)))
    kb = kp.reshape(B, H, n_blocks, MOBA_BLOCK, dh)
    vb = vp.reshape(B, H, n_blocks, MOBA_BLOCK, dh)
    k_mean = jnp.mean(kb.astype(jnp.float32), axis=3)
    q_block = jnp.arange(S) // MOBA_BLOCK
    past = jnp.arange(n_blocks)[None, :] < q_block[:, None]
    gate = jnp.einsum('bhsd,bhnd->bhsn', q.astype(jnp.float32), k_mean)
    gate = jnp.where(past, gate, -jnp.inf)
    _, sel_idx = lax.top_k(gate, n_sel)
    sel_ok = sel_idx < q_block[:, None]

    n_chunks = S // MOBA_QUERY_CHUNK

    def to_chunks(t):
        return jnp.moveaxis(t.reshape(B, H, n_chunks, MOBA_QUERY_CHUNK, *t.shape[3:]), 2, 0)

    b_ix = jnp.arange(B)[:, None, None, None]
    h_ix = jnp.arange(H)[None, :, None, None]
    scale = HEAD_DIM ** -0.5

    def chunk_fn(args):
        qc, idx, ok, ci = args
        q_pos = ci * MOBA_QUERY_CHUNK + jnp.arange(MOBA_QUERY_CHUNK)
        own = (ci * MOBA_QUERY_CHUNK) // MOBA_BLOCK
        kg = kb[b_ix, h_ix, idx]
        vg = vb[b_ix, h_ix, idx]
        s_sel = jnp.einsum('bhqd,bhqnkd->bhqnk', qc, kg).astype(jnp.float32) * scale
        s_sel = jnp.where(ok[..., None], s_sel, NEG_INF).reshape(B, H, MOBA_QUERY_CHUNK, n_sel * MOBA_BLOCK)
        k_own = lax.dynamic_slice_in_dim(kp, own * MOBA_BLOCK, MOBA_BLOCK, axis=2)
        v_own = lax.dynamic_slice_in_dim(vp, own * MOBA_BLOCK, MOBA_BLOCK, axis=2)
        k_pos = own * MOBA_BLOCK + jnp.arange(MOBA_BLOCK)
        s_own = jnp.einsum('bhqd,bhkd->bhqk', qc, k_own).astype(jnp.float32) * scale
        s_own = jnp.where(k_pos[None, :] <= q_pos[:, None], s_own, NEG_INF)
        p = jax.nn.softmax(jnp.concatenate([s_sel, s_own], axis=-1), axis=-1).astype(v.dtype)
        p_sel = p[..., :n_sel * MOBA_BLOCK].reshape(B, H, MOBA_QUERY_CHUNK, n_sel, MOBA_BLOCK)
        p_own = p[..., n_sel * MOBA_BLOCK:]
        return (jnp.einsum('bhqnk,bhqnkd->bhqd', p_sel, vg)
                + jnp.einsum('bhqk,bhkd->bhqd', p_own, v_own))

    out = lax.map(chunk_fn, (to_chunks(q), to_chunks(sel_idx), to_chunks(sel_ok),
                             jnp.arange(n_chunks)))
    return jnp.moveaxis(out, 0, 2).reshape(B, H, S, dh)


def blocked_causal_attention(q, k, v, scale):
    B, H, S, dq = q.shape
    dv = v.shape[-1]
    n_q = S // ATTN_QUERY_BLOCK
    qb = jnp.moveaxis(q.reshape(B, H, n_q, ATTN_QUERY_BLOCK, dq), 2, 0)
    k_pos = jnp.arange(S)

    def block_fn(args):
        qi, bi = args
        q_pos = bi * ATTN_QUERY_BLOCK + jnp.arange(ATTN_QUERY_BLOCK)
        s = jnp.einsum('bhqd,bhkd->bhqk', qi, k).astype(jnp.float32) * scale
        s = jnp.where(k_pos[None, :] <= q_pos[:, None], s, NEG_INF)
        p = jax.nn.softmax(s, axis=-1).astype(v.dtype)
        return jnp.einsum('bhqk,bhkd->bhqd', p, v)

    out = lax.map(block_fn, (qb, jnp.arange(n_q)))
    return jnp.moveaxis(out, 0, 2).reshape(B, H, S, dv)


def conformer_conv(u, conv_w, conv_b, ln_g, ln_b):
    a, g = jnp.split(u, 2, axis=-1)
    z = a * jax.nn.sigmoid(g)
    z = lax.conv_general_dilated(z, conv_w[:, None, :].astype(z.dtype), window_strides=(1,),
                                 padding=[(CONV_WIDTH - 1, 0)],
                                 dimension_numbers=('NWC', 'WIO', 'NWC'),
                                 feature_group_count=CONV_CHANNELS) + conv_b
    return jax.nn.silu(layer_norm(z, ln_g, ln_b))


def token_mixer(h, positions, w_in, moba_q_norm_g, moba_k_norm_g, conv_w, conv_b, conv_ln_g,
                conv_ln_b, mla_q_latent_g, mla_w_uq, mla_kv_latent_g, mla_w_ukv, mla_q_norm_g,
                mla_k_norm_g, mix_out_g, w_out):
    B, S, _ = h.shape
    proj = h @ w_in
    q_a, k_a, v_a, u_b, cq, ckv, k_pe = jnp.split(proj, IN_SPLITS, axis=-1)

    def heads_a(t):
        return t.reshape(B, S, MOBA_HEADS, HEAD_DIM)
    q_a = partial_rope(rms_norm(heads_a(q_a), moba_q_norm_g), positions, ROPE_DIMS, ROPE_THETA)
    k_a = partial_rope(rms_norm(heads_a(k_a), moba_k_norm_g), positions, ROPE_DIMS, ROPE_THETA)
    v_a = heads_a(v_a)
    tr = lambda t: jnp.transpose(t, (0, 2, 1, 3))
    out_a = tr(moba_attention(tr(q_a), tr(k_a), tr(v_a))).reshape(B, S, MOBA_WIDTH)

    out_b = conformer_conv(u_b, conv_w, conv_b, conv_ln_g, conv_ln_b)

    q_c = (rms_norm(cq, mla_q_latent_g) @ mla_w_uq).reshape(B, S, MLA_HEADS, MLA_QK_DIM)
    q_nope, q_rot = q_c[..., :MLA_NOPE_DIM], q_c[..., MLA_NOPE_DIM:]
    kv = (rms_norm(ckv, mla_kv_latent_g) @ mla_w_ukv).reshape(B, S, MLA_HEADS, MLA_NOPE_DIM + MLA_V_DIM)
    k_nope, v_c = kv[..., :MLA_NOPE_DIM], kv[..., MLA_NOPE_DIM:]
    k_rot = jnp.broadcast_to(k_pe[:, :, None, :], (B, S, MLA_HEADS, MLA_ROPE_DIM))
    q_c = partial_rope(rms_norm(jnp.concatenate([q_rot, q_nope], -1), mla_q_norm_g),
                       positions, MLA_ROPE_DIM, MLA_ROPE_THETA)
    k_c = partial_rope(rms_norm(jnp.concatenate([k_rot, k_nope], -1), mla_k_norm_g),
                       positions, MLA_ROPE_DIM, MLA_ROPE_THETA)
    out_c = tr(blocked_causal_attention(tr(q_c), tr(k_c), tr(v_c), MLA_QK_DIM ** -0.5))
    out_c = out_c.reshape(B, S, MLA_WIDTH)

    g_a = mix_out_g[:MOBA_WIDTH]
    g_b = mix_out_g[MOBA_WIDTH:MOBA_WIDTH + CONV_CHANNELS]
    g_c = mix_out_g[MOBA_WIDTH + CONV_CHANNELS:]
    mixed = jnp.concatenate([rms_norm(out_a, g_a), rms_norm(out_b, g_b), rms_norm(out_c, g_c)], -1)
    return mixed @ w_out


def grouped_moe(h, router_w, router_bias, w_gate, w_up, w_down):
    B, S, D = h.shape
    t = h.reshape(B * S, D)
    scores = jax.nn.sigmoid(t.astype(jnp.float32) @ router_w.astype(jnp.float32))
    biased = scores + router_bias.astype(jnp.float32)
    grouped = biased.reshape(-1, N_EXPERT_GROUPS, EXPERTS_PER_GROUP)
    group_score = jnp.sum(lax.top_k(grouped, TOP_K)[0], axis=-1)
    best = jnp.argmax(group_score, axis=-1)
    in_group = jnp.take_along_axis(grouped, best[:, None, None], axis=1)[:, 0]
    _, local = lax.top_k(in_group, TOP_K)
    expert = best[:, None] * EXPERTS_PER_GROUP + local
    w = jnp.take_along_axis(scores, expert, axis=-1)
    w = w / jnp.sum(w, axis=-1, keepdims=True)
    combine = jnp.sum(jax.nn.one_hot(expert, N_EXPERTS, dtype=jnp.float32) * w[..., None], axis=1)
    hidden = jax.nn.silu(jnp.einsum('nd,edf->nef', t, w_gate)) * jnp.einsum('nd,edf->nef', t, w_up)
    hidden = hidden * combine[:, :, None].astype(hidden.dtype)
    return jnp.einsum('nef,efd->nd', hidden, w_down).reshape(B, S, D)


def setup_inputs(seed: int = 0) -> dict:
    key = jax.random.key(seed)
    ks = jax.random.split(key, 32)
    f32 = jnp.float32

    def nrm(k, shape, scale):
        return jax.random.normal(k, shape, dtype=f32) * scale

    def gain(k, shape):
        return 1.0 + 0.02 * jax.random.normal(k, shape, dtype=f32)

    offset = jax.random.randint(ks[2], (BATCH, 1), 0, 1024, dtype=jnp.int32)
    return {
        "x": nrm(ks[0], (BATCH, SEQ, D_MODEL), 1.0),
        "c": nrm(ks[1], (BATCH, D_MODEL), 1.0),
        "positions": offset + jnp.arange(SEQ, dtype=jnp.int32)[None, :],
        "ada_w": nrm(ks[3], (DEPTH, D_MODEL, 6 * D_MODEL), 0.5 * D_MODEL ** -0.5),
        "ada_b": nrm(ks[4], (DEPTH, 6 * D_MODEL), 0.02),
        "norm_mix_g": gain(ks[5], (DEPTH, D_MODEL)),
        "norm_ffn_g": gain(ks[6], (DEPTH, D_MODEL)),
        "w_in": nrm(ks[7], (DEPTH, D_MODEL, D_IN), D_MODEL ** -0.5),
        "moba_q_norm_g": gain(ks[8], (DEPTH, HEAD_DIM)),
        "moba_k_norm_g": gain(ks[9], (DEPTH, HEAD_DIM)),
        "conv_w": nrm(ks[10], (DEPTH, CONV_WIDTH, CONV_CHANNELS), CONV_WIDTH ** -0.5),
        "conv_b": nrm(ks[11], (DEPTH, CONV_CHANNELS), 0.02),
        "conv_ln_g": gain(ks[12], (DEPTH, CONV_CHANNELS)),
        "conv_ln_b": nrm(ks[13], (DEPTH, CONV_CHANNELS), 0.02),
        "mla_q_latent_g": gain(ks[14], (DEPTH, MLA_Q_RANK)),
        "mla_w_uq": nrm(ks[15], (DEPTH, MLA_Q_RANK, MLA_HEADS * MLA_QK_DIM), MLA_Q_RANK ** -0.5),
        "mla_kv_latent_g": gain(ks[16], (DEPTH, MLA_KV_RANK)),
        "mla_w_ukv": nrm(ks[17], (DEPTH, MLA_KV_RANK, MLA_HEADS * (MLA_NOPE_DIM + MLA_V_DIM)), MLA_KV_RANK ** -0.5),
        "mla_q_norm_g": gain(ks[18], (DEPTH, MLA_QK_DIM)),
        "mla_k_norm_g": gain(ks[19], (DEPTH, MLA_QK_DIM)),
        "mix_out_g": gain(ks[20], (DEPTH, D_MIX)),
        "w_out": nrm(ks[21], (DEPTH, D_MIX, D_MODEL), D_MIX ** -0.5),
        "router_w": nrm(ks[22], (D_MODEL, N_EXPERTS), D_MODEL ** -0.5),
        "router_bias": nrm(ks[23], (N_EXPERTS,), 0.01),
        "moe_w_gate": nrm(ks[24], (DEPTH, N_EXPERTS, D_MODEL, D_EXPERT), D_MODEL ** -0.5),
        "moe_w_up": nrm(ks[25], (DEPTH, N_EXPERTS, D_MODEL, D_EXPERT), D_MODEL ** -0.5),
        "moe_w_down": nrm(ks[26], (DEPTH, N_EXPERTS, D_EXPERT, D_MODEL), D_EXPERT ** -0.5),
    }


def reference(x, c, positions, ada_w, ada_b, norm_mix_g, norm_ffn_g, w_in, moba_q_norm_g,
              moba_k_norm_g, conv_w, conv_b, conv_ln_g, conv_ln_b, mla_q_latent_g, mla_w_uq,
              mla_kv_latent_g, mla_w_ukv, mla_q_norm_g, mla_k_norm_g, mix_out_g, w_out,
              router_w, router_bias, moe_w_gate, moe_w_up, moe_w_down):
    c_act = jax.nn.silu(c)
    for l in range(DEPTH):
        mod = c_act @ ada_w[l] + ada_b[l]
        sh_m, sc_m, g_m, sh_f, sc_f, g_f = jnp.split(mod, 6, axis=-1)
        h = rms_norm(x, norm_mix_g[l]) * (1 + sc_m[:, None, :]) + sh_m[:, None, :]
        x = x + g_m[:, None, :] * token_mixer(
            h, positions, w_in[l], moba_q_norm_g[l], moba_k_norm_g[l], conv_w[l], conv_b[l],
            conv_ln_g[l], conv_ln_b[l], mla_q_latent_g[l], mla_w_uq[l], mla_kv_latent_g[l],
            mla_w_ukv[l], mla_q_norm_g[l], mla_k_norm_g[l], mix_out_g[l], w_out[l])
        h = rms_norm(x, norm_ffn_g[l]) * (1 + sc_f[:, None, :]) + sh_f[:, None, :]
        x = x + g_f[:, None, :] * grouped_moe(h, router_w, router_bias, moe_w_gate[l],
                                              moe_w_up[l], moe_w_down[l])
    return x
```

```python
import functools

import jax
import jax.numpy as jnp
from jax import lax
from jax.experimental import pallas as pl
from jax.experimental.pallas import tpu as pltpu

F32 = jnp.float32
BF16 = jnp.bfloat16
I32 = jnp.int32

NORM_EPS = 1e-6
NEG_INF = -1e30

LANES = 128
HEAD_DIM = 128
HEADS = 12
MOBA_BLOCK = 256
MOBA_TOPK = 3
MOBA_ROT = 32
MOBA_THETA = 500000.0
CONV_C = 1024
CONV_W = 31
CONV_HALO = 32
MLA_Q_RANK = 1024
MLA_KV_RANK = 512
MLA_NOPE = 128
MLA_ROT = 64
MLA_QK = MLA_NOPE + MLA_ROT
MLA_THETA = 10000.0
AUG = 256
N_EXPERTS = 16
N_GROUPS = 4
EPG = N_EXPERTS // N_GROUPS
D_EXPERT = 1024
MOBA_W = HEADS * HEAD_DIM
ROW_TILE = 256
VMEM_LIMIT = 56 * 1024 * 1024


def _params(n_axes, vmem=None):
    return pltpu.CompilerParams(dimension_semantics=("arbitrary",) * n_axes,
                                vmem_limit_bytes=vmem)


def _split_bf16(x):
    hi = x.astype(BF16)
    lo = (x - hi.astype(F32)).astype(BF16)
    return hi, lo


def _dot3(a, b, dims):
    ah, al = _split_bf16(a)
    bh, bl = _split_bf16(b)
    d = lambda p, q: lax.dot_general(p, q, dims, preferred_element_type=F32)
    return d(ah, bh) + (d(al, bh) + d(ah, bl))


_NN = (((1,), (0,)), ((), ()))
_NT = (((1,), (1,)), ((), ()))


def _adaln_kernel(c_ref, w_ref, b_ref, o_ref):
    c = c_ref[...]
    ca = (c * jax.nn.sigmoid(c)).astype(BF16)
    o_ref[...] = jnp.dot(ca, w_ref[...].astype(BF16), preferred_element_type=F32) + b_ref[...]


def _adaln(c, ada_w, ada_b):
    depth, d, d6 = ada_w.shape
    b = c.shape[0]
    tn = 512
    cp = jnp.zeros((8, d), F32).at[:b].set(c)
    out = pl.pallas_call(
        _adaln_kernel,
        grid=(depth, d6 // tn),
        in_specs=[pl.BlockSpec((8, d), lambda l, n: (0, 0)),
                  pl.BlockSpec((None, d, tn), lambda l, n: (l, 0, n)),
                  pl.BlockSpec((None, 1, tn), lambda l, n: (l, 0, n))],
        out_specs=pl.BlockSpec((None, 8, tn), lambda l, n: (l, 0, n)),
        out_shape=jax.ShapeDtypeStruct((depth, 8, d6), F32),
        compiler_params=_params(2, VMEM_LIMIT),
        name="adaln",
    )(cp, ada_w, ada_b.reshape(depth, 1, d6))
    return out[:, :b]


def _rope_kernel(pos_ref, k_ref, ca, spa, sma, cc, spc, smc):
    p = pos_ref[...]
    for base, oc, op, om in ((0, ca, spa, sma), (3, cc, spc, smc)):
        ang = p * k_ref[base:base + 1, :]
        si = jnp.sin(ang)
        oc[...] = jnp.cos(ang)
        op[...] = si * k_ref[base + 1:base + 2, :]
        om[...] = si * k_ref[base + 2:base + 3, :]


def _rope_consts():
    lane = jnp.arange(LANES)

    def rows(rot, theta):
        half = rot // 2
        inv_freq = jnp.float32(theta) ** (-jnp.arange(half, dtype=F32) * (2.0 / rot))
        f = jnp.where(lane < rot, inv_freq[lane % half], 0.0)
        mp = ((lane >= half) & (lane < rot)).astype(F32)
        mm = -(lane < half).astype(F32)
        return [f, mp, mm]

    z = jnp.zeros((LANES,), F32)
    return jnp.stack(rows(MOBA_ROT, MOBA_THETA) + rows(MLA_ROT, MLA_THETA) + [z, z]).astype(F32)


def _rope_tables(positions):
    n = positions.size
    posb = jnp.broadcast_to(positions.reshape(n, 1).astype(F32), (n, LANES))
    t = ROW_TILE
    spec = pl.BlockSpec((t, LANES), lambda i: (i, 0))
    return pl.pallas_call(
        _rope_kernel,
        grid=(n // t,),
        in_specs=[spec, pl.BlockSpec((8, LANES), lambda i: (0, 0))],
        out_specs=[spec] * 6,
        out_shape=[jax.ShapeDtypeStruct((n, LANES), F32)] * 6,
        compiler_params=_params(1),
        name="rope_tables",
    )(posb, _rope_consts())


def _rope(y, c, sp, sm, half):
    return y * c + pltpu.roll(y, half, 1) * sp + pltpu.roll(y, LANES - half, 1) * sm


def _mod_norm(x, g, sc, sh):
    y = x * lax.rsqrt(jnp.mean(x * x, axis=-1, keepdims=True) + NORM_EPS)
    return (y * g) * (1.0 + sc) + sh


def _norm_mod_kernel(x_ref, g_ref, sc_ref, sh_ref, o_ref):
    o_ref[...] = _mod_norm(x_ref[...], g_ref[...], sc_ref[...], sh_ref[...]).astype(BF16)


def _norm_mod(x, g, sc, sh, seq):
    n, d = x.shape
    t = ROW_TILE
    per_b = seq // t
    bspec = pl.BlockSpec((None, 1, d), lambda i: (i // per_b, 0, 0))
    return pl.pallas_call(
        _norm_mod_kernel,
        grid=(n // t,),
        in_specs=[pl.BlockSpec((t, d), lambda i: (i, 0)),
                  pl.BlockSpec((1, d), lambda i: (0, 0)), bspec, bspec],
        out_specs=pl.BlockSpec((t, d), lambda i: (i, 0)),
        out_shape=jax.ShapeDtypeStruct((n, d), BF16),
        compiler_params=_params(1),
        name="norm_mod",
    )(x, g.reshape(1, d), sc[:, None, :], sh[:, None, :])


def _mm_kernel(a_ref, w_ref, o_ref, wbf_ref):
    @pl.when(pl.program_id(1) == 0)
    def _():
        wbf_ref[...] = w_ref[...].astype(BF16)

    o_ref[...] = jnp.dot(a_ref[...], wbf_ref[...], preferred_element_type=F32).astype(o_ref.dtype)


def _mm_res_kernel(a_ref, w_ref, r_ref, g_ref, o_ref, wbf_ref):
    @pl.when(pl.program_id(1) == 0)
    def _():
        wbf_ref[...] = w_ref[...].astype(BF16)

    acc = jnp.dot(a_ref[...], wbf_ref[...], preferred_element_type=F32)
    o_ref[...] = r_ref[...] + g_ref[...] * acc


def _matmul(a, w, layer, col_off, ncols, tn, out_dtype, tm=512):
    m, k = a.shape
    tm = min(tm, m)
    off = col_off // tn
    return pl.pallas_call(
        _mm_kernel,
        grid=(ncols // tn, m // tm),
        in_specs=[pl.BlockSpec((tm, k), lambda n, i: (i, 0)),
                  pl.BlockSpec((None, k, tn), lambda n, i: (layer, 0, n + off))],
        out_specs=pl.BlockSpec((tm, tn), lambda n, i: (i, n)),
        out_shape=jax.ShapeDtypeStruct((m, ncols), out_dtype),
        scratch_shapes=[pltpu.VMEM((k, tn), BF16)],
        compiler_params=_params(2, VMEM_LIMIT),
        name="matmul",
    )(a, w)


def _matmul_res(a, w, layer, res, gate, seq, tn, tm=512):
    m, k = a.shape
    ncols = w.shape[-1]
    tm = min(tm, seq)
    per_b = seq // tm
    return pl.pallas_call(
        _mm_res_kernel,
        grid=(ncols // tn, m // tm),
        in_specs=[pl.BlockSpec((tm, k), lambda n, i: (i, 0)),
                  pl.BlockSpec((None, k, tn), lambda n, i: (layer, 0, n)),
                  pl.BlockSpec((tm, tn), lambda n, i: (i, n)),
                  pl.BlockSpec((None, 1, tn), lambda n, i: (i // per_b, 0, n))],
        out_specs=pl.BlockSpec((tm, tn), lambda n, i: (i, n)),
        out_shape=jax.ShapeDtypeStruct((m, ncols), F32),
        scratch_shapes=[pltpu.VMEM((k, tn), BF16)],
        compiler_params=_params(2, VMEM_LIMIT),
        name="matmul_res",
    )(a, w, res, gate[:, None, :])


def _head_norm(x, g):
    return x * lax.rsqrt(jnp.mean(x * x, axis=-1, keepdims=True) + NORM_EPS) * g


def _moba_k_kernel(k_ref, g_ref, c_ref, sp_ref, sm_ref, ka_ref, km_ref, *, nblk):
    blk = pl.program_id(0) % nblk
    t = k_ref.shape[0]
    lane = lax.broadcasted_iota(I32, (t, LANES), 1)
    onehot = (lane == blk).astype(BF16)
    c, sp, sm = c_ref[...], sp_ref[...], sm_ref[...]
    for h in range(HEADS):
        y = _head_norm(k_ref[:, h * HEAD_DIM:(h + 1) * HEAD_DIM], g_ref[...])
        y = _rope(y, c, sp, sm, MOBA_ROT // 2)
        ka_ref[:, h * AUG:h * AUG + HEAD_DIM] = y.astype(BF16)
        ka_ref[:, h * AUG + HEAD_DIM:(h + 1) * AUG] = onehot
        km_ref[h:h + 1, :] = jnp.mean(y, axis=0, keepdims=True)


def _moba_q_kernel(q_ref, g_ref, c_ref, sp_ref, sm_ref, km_ref, qa_ref, *, nblk, scale):
    blk = pl.program_id(0) % nblk
    t = q_ref.shape[0]
    lane = lax.broadcasted_iota(I32, (t, LANES), 1)
    past = lane < blk
    own = lane == blk
    lane_f = lane.astype(F32)
    c, sp, sm = c_ref[...], sp_ref[...], sm_ref[...]
    for h in range(HEADS):
        y = _head_norm(q_ref[:, h * HEAD_DIM:(h + 1) * HEAD_DIM], g_ref[...])
        y = _rope(y, c, sp, sm, MOBA_ROT // 2)
        gate = _dot3(y, km_ref[h], _NT)
        g = jnp.where(past, gate, -jnp.inf)
        sel = own
        for _ in range(MOBA_TOPK):
            m = jnp.max(g, axis=-1, keepdims=True)
            idx = jnp.min(jnp.where(g == m, lane_f, float(LANES)), axis=-1, keepdims=True)
            pick = (lane_f == idx) & (m > -jnp.inf)
            sel = sel | pick
            g = jnp.where(pick, -jnp.inf, g)
        bias = jnp.where(sel, 0.0, NEG_INF)
        qa_ref[:, h * AUG:h * AUG + HEAD_DIM] = (y * scale).astype(BF16)
        qa_ref[:, h * AUG + HEAD_DIM:(h + 1) * AUG] = bias.astype(BF16)


def _moba_prep(q, k, gq, gk, tabs, batch, seq):
    n = q.shape[0]
    t = MOBA_BLOCK
    nblk = seq // t
    c, sp, sm = tabs
    row = lambda w: pl.BlockSpec((t, w), lambda i: (i, 0))
    gspec = pl.BlockSpec((1, HEAD_DIM), lambda i: (0, 0))
    k_aug, kmean = pl.pallas_call(
        functools.partial(_moba_k_kernel, nblk=nblk),
        grid=(n // t,),
        in_specs=[row(MOBA_W), gspec, row(LANES), row(LANES), row(LANES)],
        out_specs=[row(HEADS * AUG), pl.BlockSpec((None, HEADS, HEAD_DIM), lambda i: (i, 0, 0))],
        out_shape=[jax.ShapeDtypeStruct((n, HEADS * AUG), BF16),
                   jax.ShapeDtypeStruct((n // t, HEADS, HEAD_DIM), F32)],
        compiler_params=_params(1),
        name="moba_k_prep",
    )(k, gk.reshape(1, HEAD_DIM), c, sp, sm)
    km = kmean.reshape(batch, nblk, HEADS, HEAD_DIM).transpose(0, 2, 1, 3)
    km = jnp.pad(km, ((0, 0), (0, 0), (0, LANES - nblk), (0, 0)))
    q_aug = pl.pallas_call(
        functools.partial(_moba_q_kernel, nblk=nblk, scale=HEAD_DIM ** -0.5),
        grid=(n // t,),
        in_specs=[row(MOBA_W), gspec, row(LANES), row(LANES), row(LANES),
                  pl.BlockSpec((None, HEADS, LANES, HEAD_DIM), lambda i: (i // nblk, 0, 0, 0))],
        out_specs=row(HEADS * AUG),
        out_shape=jax.ShapeDtypeStruct((n, HEADS * AUG), BF16),
        compiler_params=_params(1),
        name="moba_q_prep",
    )(q, gq.reshape(1, HEAD_DIM), c, sp, sm, km)
    return q_aug, k_aug


def _attn_kernel(q_ref, k_ref, v_ref, o_ref, *, t):
    qi = pl.program_id(2)
    q = q_ref[...]
    ones = jnp.ones((t, HEAD_DIM), BF16)

    def scores(j):
        kj = k_ref[pl.ds(pl.multiple_of(j * t, t), t), :]
        return lax.dot_general(q, kj, _NT, preferred_element_type=F32)

    def pv(p, j):
        vj = v_ref[pl.ds(pl.multiple_of(j * t, t), t), :]
        return jnp.dot(p.astype(BF16), jnp.concatenate([vj, ones], axis=1),
                       preferred_element_type=F32)

    s = scores(qi)
    row = lax.broadcasted_iota(I32, (t, t), 0)
    col = lax.broadcasted_iota(I32, (t, t), 1)
    s = jnp.where(col <= row, s, NEG_INF)
    m = jnp.max(s, axis=-1, keepdims=True)
    acc = pv(jnp.exp(s - m), qi)

    def body(j, carry):
        m, acc = carry
        s = scores(j)
        m_new = jnp.maximum(m, jnp.max(s, axis=-1, keepdims=True))
        acc = jnp.exp(m - m_new) * acc + pv(jnp.exp(s - m_new), j)
        return m_new, acc

    m, acc = lax.fori_loop(0, qi, body, (m, acc))
    o_ref[...] = acc[:, :HEAD_DIM] / acc[:, HEAD_DIM:]


def _attention(q_aug, k_aug, v, batch, seq):
    n = q_aug.shape[0]
    t = MOBA_BLOCK
    nq = seq // t
    return pl.pallas_call(
        functools.partial(_attn_kernel, t=t),
        grid=(batch, HEADS, nq),
        in_specs=[pl.BlockSpec((t, AUG), lambda b, h, i: (b * nq + i, h)),
                  pl.BlockSpec((seq, AUG), lambda b, h, i: (b, h)),
                  pl.BlockSpec((seq, HEAD_DIM), lambda b, h, i: (b, h))],
        out_specs=pl.BlockSpec((t, HEAD_DIM), lambda b, h, i: (b * nq + i, h)),
        out_shape=jax.ShapeDtypeStruct((n, HEADS * HEAD_DIM), F32),
        compiler_params=_params(3),
        name="attention",
    )(q_aug, k_aug, v)


def _conv_kernel(u_ref, uh_ref, cw_ref, cb_ref, lg_ref, lb_ref, o_ref, zs_ref, *, nblk):
    blk = pl.program_id(0) % nblk
    t = u_ref.shape[0]
    uh = uh_ref[...]
    zh = uh[:, :CONV_C] * jax.nn.sigmoid(uh[:, CONV_C:])
    zs_ref[0:CONV_HALO, :] = jnp.where(blk > 0, zh, 0.0)
    u = u_ref[...]
    zs_ref[CONV_HALO:, :] = u[:, :CONV_C] * jax.nn.sigmoid(u[:, CONV_C:])
    first = CONV_HALO - (CONV_W - 1)
    rc, lc = 64, 256
    for r0 in range(0, t, rc):
        for c0 in range(0, CONV_C, lc):
            acc = jnp.zeros((rc, lc), F32)
            for w in range(CONV_W):
                acc = acc + zs_ref[r0 + first + w:r0 + first + w + rc, c0:c0 + lc] * cw_ref[w:w + 1, c0:c0 + lc]
            o_ref[r0:r0 + rc, c0:c0 + lc] = acc + cb_ref[:, c0:c0 + lc]
    y = o_ref[...]
    mu = jnp.mean(y, axis=-1, keepdims=True)
    var = jnp.mean(jnp.square(y - mu), axis=-1, keepdims=True)
    yn = (y - mu) * lax.rsqrt(var + NORM_EPS) * lg_ref[...] + lb_ref[...]
    o_ref[...] = yn * jax.nn.sigmoid(yn)


def _conv_module(u, conv_w, conv_b, ln_g, ln_b, seq):
    n = u.shape[0]
    t = ROW_TILE
    nblk = seq // t
    per = t // CONV_HALO
    vec = pl.BlockSpec((1, CONV_C), lambda i: (0, 0))
    cw = jnp.pad(conv_w, ((0, 32 - CONV_W), (0, 0)))
    return pl.pallas_call(
        functools.partial(_conv_kernel, nblk=nblk),
        grid=(n // t,),
        in_specs=[pl.BlockSpec((t, 2 * CONV_C), lambda i: (i, 0)),
                  pl.BlockSpec((CONV_HALO, 2 * CONV_C), lambda i: (jnp.maximum(i * per - 1, 0), 0)),
                  pl.BlockSpec((32, CONV_C), lambda i: (0, 0)), vec, vec, vec],
        out_specs=pl.BlockSpec((t, CONV_C), lambda i: (i, 0)),
        out_shape=jax.ShapeDtypeStruct((n, CONV_C), F32),
        scratch_shapes=[pltpu.VMEM((CONV_HALO + t, CONV_C), F32)],
        compiler_params=_params(1),
        name="conv_module",
    )(u, u, cw, conv_b.reshape(1, -1), ln_g.reshape(1, -1), ln_b.reshape(1, -1))


def _mla_q_kernel(cq_ref, gl_ref, w_ref, gq_ref, c_ref, sp_ref, sm_ref, qa_ref, *, scale):
    x = cq_ref[...]
    xn = (x * lax.rsqrt(jnp.mean(x * x, axis=-1, keepdims=True) + NORM_EPS) * gl_ref[...]).astype(BF16)
    q = jnp.dot(xn, w_ref[...], preferred_element_type=F32)
    c, sp, sm = c_ref[...], sp_ref[...], sm_ref[...]
    for h in range(HEADS):
        qh = q[:, h * AUG:(h + 1) * AUG]
        r = lax.rsqrt(jnp.sum(qh * qh, axis=-1, keepdims=True) * (1.0 / MLA_QK) + NORM_EPS)
        y = qh * r * gq_ref[...]
        qa_ref[:, h * AUG:h * AUG + MLA_NOPE] = (y[:, :MLA_NOPE] * scale).astype(BF16)
        yr = _rope(y[:, MLA_NOPE:], c, sp, sm, MLA_ROT // 2)
        qa_ref[:, h * AUG + MLA_NOPE:(h + 1) * AUG] = (yr * scale).astype(BF16)


def _mla_kv_kernel(ckv_ref, kpe_ref, gl_ref, w_ref, gk_ref, c_ref, sp_ref, sm_ref, ka_ref, v_ref):
    x = ckv_ref[...]
    xn = (x * lax.rsqrt(jnp.mean(x * x, axis=-1, keepdims=True) + NORM_EPS) * gl_ref[...]).astype(BF16)
    kv = jnp.dot(xn, w_ref[...], preferred_element_type=F32)
    kpe = kpe_ref[...]
    ss_pe = jnp.sum(kpe * kpe, axis=-1, keepdims=True)
    g_nope = gk_ref[:, :MLA_NOPE]
    kr = _rope(kpe * gk_ref[:, MLA_NOPE:], c_ref[...], sp_ref[...], sm_ref[...], MLA_ROT // 2)
    for h in range(HEADS):
        kn = kv[:, h * AUG:h * AUG + MLA_NOPE]
        r = lax.rsqrt((jnp.sum(kn * kn, axis=-1, keepdims=True) + ss_pe) * (1.0 / MLA_QK) + NORM_EPS)
        ka_ref[:, h * AUG:h * AUG + MLA_NOPE] = (kn * r * g_nope).astype(BF16)
        ka_ref[:, h * AUG + MLA_NOPE:(h + 1) * AUG] = (kr * r).astype(BF16)
        v_ref[:, h * HEAD_DIM:(h + 1) * HEAD_DIM] = kv[:, h * AUG + MLA_NOPE:(h + 1) * AUG].astype(BF16)


def _pad_rot(g):
    return jnp.concatenate([g[MLA_ROT:], g[:MLA_ROT], jnp.zeros((AUG - MLA_QK,), F32)]).reshape(1, AUG)


def _mla_prep(cq, ckv, kpe, g_ql, w_uq, g_kvl, w_ukv, g_q, g_k, tabs):
    n = cq.shape[0]
    t = ROW_TILE
    c, sp, sm = tabs
    wq = w_uq.reshape(MLA_Q_RANK, HEADS, MLA_QK)
    wq = jnp.pad(wq, ((0, 0), (0, 0), (0, AUG - MLA_QK))).reshape(MLA_Q_RANK, HEADS * AUG).astype(BF16)
    wkv = w_ukv.astype(BF16)
    row = lambda w: pl.BlockSpec((t, w), lambda i: (i, 0))
    full = lambda a, b: pl.BlockSpec((a, b), lambda i: (0, 0))
    q_aug = pl.pallas_call(
        functools.partial(_mla_q_kernel, scale=MLA_QK ** -0.5),
        grid=(n // t,),
        in_specs=[row(MLA_Q_RANK), full(1, MLA_Q_RANK), full(MLA_Q_RANK, HEADS * AUG), full(1, AUG),
                  row(LANES), row(LANES), row(LANES)],
        out_specs=row(HEADS * AUG),
        out_shape=jax.ShapeDtypeStruct((n, HEADS * AUG), BF16),
        compiler_params=_params(1, VMEM_LIMIT),
        name="mla_q_prep",
    )(cq, g_ql.reshape(1, -1), wq, _pad_rot(g_q), c, sp, sm)
    k_aug, v = pl.pallas_call(
        _mla_kv_kernel,
        grid=(n // t,),
        in_specs=[row(MLA_KV_RANK), row(LANES), full(1, MLA_KV_RANK), full(MLA_KV_RANK, HEADS * AUG),
                  full(1, AUG), row(LANES), row(LANES), row(LANES)],
        out_specs=[row(HEADS * AUG), row(HEADS * HEAD_DIM)],
        out_shape=[jax.ShapeDtypeStruct((n, HEADS * AUG), BF16),
                   jax.ShapeDtypeStruct((n, HEADS * HEAD_DIM), BF16)],
        compiler_params=_params(1, VMEM_LIMIT),
        name="mla_kv_prep",
    )(ckv, kpe, g_kvl.reshape(1, -1), wkv, _pad_rot(g_k), c, sp, sm)
    return q_aug, k_aug, v


def _mixnorm_kernel(a_ref, b_ref, c_ref, g_ref, o_ref):
    off = 0
    for ref in (a_ref, b_ref, c_ref):
        x = ref[...]
        w = x.shape[1]
        y = x * lax.rsqrt(jnp.mean(x * x, axis=-1, keepdims=True) + NORM_EPS) * g_ref[:, off:off + w]
        o_ref[:, off:off + w] = y.astype(BF16)
        off += w


def _mixnorm(a, b, c, g):
    n = a.shape[0]
    t = ROW_TILE
    wt = a.shape[1] + b.shape[1] + c.shape[1]
    row = lambda w: pl.BlockSpec((t, w), lambda i: (i, 0))
    return pl.pallas_call(
        _mixnorm_kernel,
        grid=(n // t,),
        in_specs=[row(a.shape[1]), row(b.shape[1]), row(c.shape[1]), pl.BlockSpec((1, wt), lambda i: (0, 0))],
        out_specs=row(wt),
        out_shape=jax.ShapeDtypeStruct((n, wt), BF16),
        compiler_params=_params(1),
        name="mixnorm",
    )(a, b, c, g.reshape(1, wt))


def _first_max(rows):
    m = rows[0]
    for r in rows[1:]:
        m = jnp.maximum(m, r)
    idx = jnp.full(m.shape, len(rows) - 1, I32)
    for i in range(len(rows) - 2, -1, -1):
        idx = jnp.where(rows[i] == m, i, idx)
    return m, idx


def _router_kernel(x_ref, g_ref, sc_ref, sh_ref, rwh_ref, rwl_ref, rb_ref, h_ref, eid_ref, wt_ref):
    hf = _mod_norm(x_ref[...], g_ref[...], sc_ref[...], sh_ref[...])
    h_ref[...] = hf.astype(BF16)
    hh, hl = _split_bf16(hf)
    d = lambda p, q: jnp.dot(p, q, preferred_element_type=F32)
    logits = d(hh, rwh_ref[...]) + (d(hl, rwh_ref[...]) + d(hh, rwl_ref[...]))
    scores = jax.nn.sigmoid(logits)
    st = scores.T
    bt = (scores + rb_ref[...]).T
    srow = [st[e:e + 1, :] for e in range(N_EXPERTS)]
    brow = [bt[e:e + 1, :] for e in range(N_EXPERTS)]
    gs, i1s, i2s = [], [], []
    for g in range(N_GROUPS):
        rows = brow[g * EPG:(g + 1) * EPG]
        m1, i1 = _first_max(rows)
        rest = [jnp.where(i1 == i, -jnp.inf, r) for i, r in enumerate(rows)]
        m2, i2 = _first_max(rest)
        gs.append(m1 + m2)
        i1s.append(i1)
        i2s.append(i2)
    _, best = _first_max(gs)
    pick = lambda vals: functools.reduce(lambda acc, gv: jnp.where(best == gv[0], gv[1], acc),
                                         list(enumerate(vals))[1:], vals[0])
    e1 = best * EPG + pick(i1s)
    e2 = best * EPG + pick(i2s)
    zero = jnp.zeros_like(srow[0])
    s1 = functools.reduce(lambda acc, e: jnp.where(e1 == e, srow[e], acc), range(N_EXPERTS), zero)
    s2 = functools.reduce(lambda acc, e: jnp.where(e2 == e, srow[e], acc), range(N_EXPERTS), zero)
    tot = s1 + s2
    eid_ref[0:1, :] = e1
    eid_ref[1:2, :] = e2
    wt_ref[0:1, :] = s1 / tot
    wt_ref[1:2, :] = s2 / tot


def _norm_router(x, g, sc, sh, router_w, router_bias, seq):
    n, d = x.shape
    t = ROW_TILE
    per_b = seq // t
    rw = jnp.pad(router_w, ((0, 0), (0, LANES - N_EXPERTS)))
    rwh, rwl = _split_bf16(rw)
    rb = jnp.pad(router_bias, (0, LANES - N_EXPERTS)).reshape(1, LANES)
    bspec = pl.BlockSpec((None, 1, d), lambda i: (i // per_b, 0, 0))
    wspec = pl.BlockSpec((d, LANES), lambda i: (0, 0))
    return pl.pallas_call(
        _router_kernel,
        grid=(n // t,),
        in_specs=[pl.BlockSpec((t, d), lambda i: (i, 0)), pl.BlockSpec((1, d), lambda i: (0, 0)),
                  bspec, bspec, wspec, wspec, pl.BlockSpec((1, LANES), lambda i: (0, 0))],
        out_specs=[pl.BlockSpec((t, d), lambda i: (i, 0)),
                   pl.BlockSpec((2, t), lambda i: (0, i)), pl.BlockSpec((2, t), lambda i: (0, i))],
        out_shape=[jax.ShapeDtypeStruct((n, d), BF16), jax.ShapeDtypeStruct((2, n), I32),
                   jax.ShapeDtypeStruct((2, n), F32)],
        compiler_params=_params(1),
        name="norm_router",
    )(x, g.reshape(1, d), sc[:, None, :], sh[:, None, :], rwh, rwl, rb)


MOE_TM = 256
GATHER_ROWS = 256
COMBINE_ROWS = 128


def _moe_plan(eid, n_chunks_up, n_chunks_down):
    n = eid.shape[1]
    tm = MOE_TM
    t_max = (2 * n) // tm + N_EXPERTS
    e_flat = eid.reshape(-1)
    onehot = (e_flat[:, None] == jnp.arange(N_EXPERTS)[None, :]).astype(I32)
    rank = jnp.sum((jnp.cumsum(onehot, axis=0) - onehot) * onehot, axis=1)
    counts = jnp.sum(onehot, axis=0)
    ntiles = (counts + tm - 1) // tm
    tile_end = jnp.cumsum(ntiles)
    tile_start = tile_end - ntiles
    pos = (tile_start[e_flat] * tm + rank).astype(I32)
    src = jnp.zeros((t_max * tm,), I32).at[pos].set((jnp.arange(2 * n) % n).astype(I32))

    used = tile_end[-1]
    unused = jnp.maximum(t_max - used, 1)

    def items(nc):
        per_e = nc * ntiles
        end = jnp.cumsum(per_e)
        total = end[-1]
        i = jnp.arange(nc * t_max)
        ic = jnp.minimum(i, total - 1)
        e = jnp.minimum(jnp.sum((ic[:, None] >= end[None, :]).astype(I32), axis=1), N_EXPERTS - 1)
        local = ic - (end[e] - per_e[e])
        nt = jnp.maximum(ntiles[e], 1)
        cw = (local // nt).astype(I32)
        spare = jnp.maximum(i - total, 0)
        live = i < total
        t = jnp.where(live, tile_start[e] + local % nt, used + spare % unused).astype(I32)
        co = jnp.where(live, cw, spare // unused).astype(I32)
        changed = jnp.concatenate([jnp.ones((1,), bool), (e[1:] != e[:-1]) | (cw[1:] != cw[:-1])])
        flag = jnp.where(live, jnp.where(changed, 2, 1), 0).astype(I32)
        return e, cw, t, co, flag

    return pos, src, items(n_chunks_up), items(n_chunks_down)


def _gather_kernel(src_ref, h_hbm, xs_hbm, sem, *, rows):
    base = pl.program_id(0) * rows

    def copy(r):
        return pltpu.make_async_copy(h_hbm.at[pl.ds(src_ref[base + r], 1)],
                                     xs_hbm.at[pl.ds(base + r, 1)], sem)

    def start(r, carry):
        copy(r).start()
        return carry

    def wait(r, carry):
        copy(r).wait()
        return carry

    lax.fori_loop(0, rows, start, 0)
    lax.fori_loop(0, rows, wait, 0)


def _moe_gather(h, src):
    p = src.shape[0]
    n, d = h.shape
    h3 = h.reshape(n, d // LANES, LANES)
    xs3 = pl.pallas_call(
        functools.partial(_gather_kernel, rows=GATHER_ROWS),
        grid_spec=pltpu.PrefetchScalarGridSpec(
            num_scalar_prefetch=1, grid=(p // GATHER_ROWS,),
            in_specs=[pl.BlockSpec(memory_space=pl.ANY)],
            out_specs=pl.BlockSpec(memory_space=pl.ANY),
            scratch_shapes=[pltpu.SemaphoreType.DMA(())]),
        out_shape=jax.ShapeDtypeStruct((p, d // LANES, LANES), h.dtype),
        compiler_params=_params(1),
        name="moe_gather",
    )(src, h3)
    return xs3.reshape(p, d)


def _moe_up_kernel(e_ref, cw_ref, t_ref, co_ref, flag_ref, x_ref, wg_ref, wu_ref, o_ref, wgs, wus):
    flag = flag_ref[pl.program_id(0)]

    @pl.when(flag == 2)
    def _():
        wgs[...] = wg_ref[...].astype(BF16)
        wus[...] = wu_ref[...].astype(BF16)

    @pl.when(flag > 0)
    def _():
        x = x_ref[...]
        g = jnp.dot(x, wgs[...], preferred_element_type=F32)
        u = jnp.dot(x, wus[...], preferred_element_type=F32)
        o_ref[...] = (g * jax.nn.sigmoid(g) * u).astype(BF16)

    @pl.when(flag == 0)
    def _():
        o_ref[...] = jnp.zeros_like(o_ref)


def _moe_down_kernel(e_ref, cw_ref, t_ref, co_ref, flag_ref, h_ref, wd_ref, o_ref, wds):
    flag = flag_ref[pl.program_id(0)]

    @pl.when(flag == 2)
    def _():
        wds[...] = wd_ref[...].astype(BF16)

    @pl.when(flag > 0)
    def _():
        o_ref[...] = jnp.dot(h_ref[...], wds[...], preferred_element_type=F32)

    @pl.when(flag == 0)
    def _():
        o_ref[...] = jnp.zeros_like(o_ref)


def _moe_experts(xs, w_gate, w_up, w_down, layer, items_up, items_down, tf, tn):
    p, d = xs.shape
    f = w_gate.shape[-1]
    tm = MOE_TM
    n_up = items_up[0].shape[0]
    n_down = items_down[0].shape[0]
    rows = lambda w: pl.BlockSpec((tm, w), lambda i, e, cw, t, co, fl: (t[i], 0))
    wspec = lambda k, w: pl.BlockSpec((None, None, k, w), lambda i, e, cw, t, co, fl: (layer, e[i], 0, cw[i]))
    ospec = lambda w: pl.BlockSpec((tm, w), lambda i, e, cw, t, co, fl: (t[i], co[i]))
    hidden = pl.pallas_call(
        _moe_up_kernel,
        grid_spec=pltpu.PrefetchScalarGridSpec(
            num_scalar_prefetch=5, grid=(n_up,),
            in_specs=[rows(d), wspec(d, tf), wspec(d, tf)],
            out_specs=ospec(tf),
            scratch_shapes=[pltpu.VMEM((d, tf), BF16), pltpu.VMEM((d, tf), BF16)]),
        out_shape=jax.ShapeDtypeStruct((p, f), BF16),
        compiler_params=_params(1, VMEM_LIMIT),
        name="moe_up",
    )(*items_up, xs, w_gate, w_up)
    return pl.pallas_call(
        _moe_down_kernel,
        grid_spec=pltpu.PrefetchScalarGridSpec(
            num_scalar_prefetch=5, grid=(n_down,),
            in_specs=[rows(f), wspec(f, tn)],
            out_specs=ospec(tn),
            scratch_shapes=[pltpu.VMEM((f, tn), BF16)]),
        out_shape=jax.ShapeDtypeStruct((p, d), F32),
        compiler_params=_params(1, VMEM_LIMIT),
        name="moe_down",
    )(*items_down, hidden, w_down)


def _combine_kernel(pos_ref, y_hbm, x_ref, g_ref, w_ref, o_ref, buf, sem, *, rows, n):
    base = pl.program_id(0) * rows

    def copy(k, r):
        return pltpu.make_async_copy(y_hbm.at[pl.ds(pos_ref[k * n + base + r], 1)],
                                     buf.at[k, pl.ds(r, 1)], sem)

    def start(r, carry):
        copy(0, r).start()
        copy(1, r).start()
        return carry

    def wait(r, carry):
        copy(0, r).wait()
        copy(1, r).wait()
        return carry

    lax.fori_loop(0, rows, start, 0)
    lax.fori_loop(0, rows, wait, 0)
    w = w_ref[...]
    w0, w1 = w[:, 0:1], w[:, 1:2]
    for j in range(buf.shape[2]):
        cols = slice(j * LANES, (j + 1) * LANES)
        moe = buf[0, :, j, :] * w0 + buf[1, :, j, :] * w1
        o_ref[:, cols] = x_ref[:, cols] + g_ref[:, cols] * moe


def _moe_combine(y, pos, x, gate, wts, seq):
    n, d = x.shape
    rows = COMBINE_ROWS
    per_b = seq // rows
    y3 = y.reshape(y.shape[0], d // LANES, LANES)
    return pl.pallas_call(
        functools.partial(_combine_kernel, rows=rows, n=n),
        grid_spec=pltpu.PrefetchScalarGridSpec(
            num_scalar_prefetch=1, grid=(n // rows,),
            in_specs=[pl.BlockSpec(memory_space=pl.ANY),
                      pl.BlockSpec((rows, d), lambda i, p: (i, 0)),
                      pl.BlockSpec((None, 1, d), lambda i, p: (i // per_b, 0, 0)),
                      pl.BlockSpec((rows, 2), lambda i, p: (i, 0))],
            out_specs=pl.BlockSpec((rows, d), lambda i, p: (i, 0)),
            scratch_shapes=[pltpu.VMEM((2, rows, d // LANES, LANES), F32), pltpu.SemaphoreType.DMA(())]),
        out_shape=jax.ShapeDtypeStruct((n, d), F32),
        compiler_params=_params(1, VMEM_LIMIT),
        name="moe_combine",
    )(pos, y3, x, gate[:, None, :], wts.T)


def _col_tile(width):
    for tn in (512, 256, 128):
        if width % tn == 0:
            return tn
    return width


def kernel(x, c, positions, ada_w, ada_b, norm_mix_g, norm_ffn_g, w_in, moba_q_norm_g, moba_k_norm_g, conv_w, conv_b, conv_ln_g, conv_ln_b, mla_q_latent_g, mla_w_uq, mla_kv_latent_g, mla_w_ukv, mla_q_norm_g, mla_k_norm_g, mix_out_g, w_out, router_w, router_bias, moe_w_gate, moe_w_up, moe_w_down):
    batch, seq, d = x.shape
    depth = ada_w.shape[0]
    n = batch * seq
    xf = x.reshape(n, d)

    mod = _adaln(c, ada_w, ada_b)
    ca, spa, sma, cc, spc, smc = _rope_tables(positions)

    o_k, o_v, o_u = MOBA_W, 2 * MOBA_W, 3 * MOBA_W
    o_cq = o_u + 2 * CONV_C
    o_ckv = o_cq + MLA_Q_RANK
    o_pe = o_ckv + MLA_KV_RANK
    tn_out = _col_tile(d)
    tf = _col_tile(D_EXPERT) if D_EXPERT <= 512 else 256
    tn_down = _col_tile(d)

    for l in range(depth):
        sh_m, sc_m, g_m, sh_f, sc_f, g_f = [mod[l, :, i * d:(i + 1) * d] for i in range(6)]

        h = _norm_mod(xf, norm_mix_g[l], sc_m, sh_m, seq)
        proj = lambda off, width, dt: _matmul(h, w_in, l, off, width, 512, dt)
        q_a = proj(0, MOBA_W, F32)
        k_a = proj(o_k, MOBA_W, F32)
        v_a = proj(o_v, MOBA_W, BF16)
        u_b = proj(o_u, 2 * CONV_C, F32)
        cq = proj(o_cq, MLA_Q_RANK, F32)
        ckv = proj(o_ckv, MLA_KV_RANK, F32)
        w_pe = jnp.pad(w_in[l, :, o_pe:], ((0, 0), (0, LANES - MLA_ROT)))[None]
        kpe = _matmul(h, w_pe, 0, 0, LANES, LANES, F32)

        qa_aug, ka_aug = _moba_prep(q_a, k_a, moba_q_norm_g[l], moba_k_norm_g[l], (ca, spa, sma), batch, seq)
        out_a = _attention(qa_aug, ka_aug, v_a, batch, seq)

        out_b = _conv_module(u_b, conv_w[l], conv_b[l], conv_ln_g[l], conv_ln_b[l], seq)

        qc_aug, kc_aug, v_c = _mla_prep(cq, ckv, kpe, mla_q_latent_g[l], mla_w_uq[l], mla_kv_latent_g[l],
                                        mla_w_ukv[l], mla_q_norm_g[l], mla_k_norm_g[l], (cc, spc, smc))
        out_c = _attention(qc_aug, kc_aug, v_c, batch, seq)

        mixed = _mixnorm(out_a, out_b, out_c, mix_out_g[l])
        xf = _matmul_res(mixed, w_out, l, xf, g_m, seq, tn_out)

        h2, eid, wts = _norm_router(xf, norm_ffn_g[l], sc_f, sh_f, router_w, router_bias, seq)
        pos, src, items_up, items_down = _moe_plan(eid, D_EXPERT // tf, d // tn_down)
        xs = _moe_gather(h2, src)
        y = _moe_experts(xs, moe_w_gate, moe_w_up, moe_w_down, l, items_up, items_down, tf, tn_down)
        xf = _moe_combine(y, pos, xf, g_f, wts, seq)

    return xf.reshape(batch, seq, d)
```
